```python
import math
import jax, jax.numpy as jnp
from jax import lax
import numpy as np

D_MODEL = 1024
BATCH = 8
SEQ = 4096
DEPTH = 1

CHUNK = 64
CONV_K = 4
EPS = 1e-6

GDN_HEADS = 8
GDN_DK = 64
GDN_DV = 64
GDN_QK = GDN_HEADS * GDN_DK
GDN_V = GDN_HEADS * GDN_DV

SSM_HEADS = 8
SSM_P = 64
SSM_N = 64
SSM_GROUPS = 2
SSM_INNER = SSM_HEADS * SSM_P
SSM_BC = SSM_GROUPS * SSM_N

D_MIX = GDN_V + SSM_INNER
D_FF = 4 * D_MODEL
N_MOD = 6

DT_MIN = 0.001
DT_MAX = 0.1

GDN_CONV_CH = 2 * GDN_QK + GDN_V
SSM_CONV_CH = SSM_INNER + 2 * SSM_BC
IN_SIZES = (GDN_CONV_CH, GDN_V, GDN_HEADS, GDN_HEADS, SSM_CONV_CH, SSM_INNER, SSM_HEADS)
D_IN_PROJ = sum(IN_SIZES)
IN_SPLITS = tuple(int(s) for s in np.cumsum(IN_SIZES)[:-1])

kernel_name = "gdn_mamba2_parallel_hybrid_adaln_block"


def rmsnorm(x, w):
    x32 = x.astype(jnp.float32)
    x32 = x32 * lax.rsqrt(jnp.mean(x32 * x32, axis=-1, keepdims=True) + EPS)
    return x32.astype(x.dtype) * w


def l2norm(x):
    x32 = x.astype(jnp.float32)
    return x32 * lax.rsqrt(jnp.sum(x32 * x32, axis=-1, keepdims=True) + EPS)


def causal_dwconv(x, w, b=None):
    y = lax.conv_general_dilated(
        x, w[:, None, :].astype(x.dtype), window_strides=(1,),
        padding=((CONV_K - 1, 0),), dimension_numbers=("NWC", "WIO", "NWC"),
        feature_group_count=x.shape[-1])
    return y if b is None else y + b


def _to_chunks(t):
    b, l, h = t.shape[:3]
    t = t.reshape((b, l // CHUNK, CHUNK, h) + t.shape[3:])
    return jnp.moveaxis(t, 3, 1)


def _from_chunks(t):
    t = jnp.moveaxis(t, 1, 3)
    return t.reshape((t.shape[0], t.shape[1] * t.shape[2]) + t.shape[3:])


def _chunk_masks():
    i = jnp.arange(CHUNK)
    return i[:, None] >= i[None, :], i[:, None] > i[None, :]


def gated_delta_rule(q, k, v, g, beta):
    f32 = jnp.float32
    q, k, v = (_to_chunks(t.astype(f32)) for t in (q, k, v))
    g, beta = _to_chunks(g.astype(f32)), _to_chunks(beta.astype(f32))
    q = q * (q.shape[-1] ** -0.5)
    incl, strict = _chunk_masks()
    G = jnp.cumsum(g, axis=-1)
    decay = jnp.exp(jnp.where(incl, G[..., :, None] - G[..., None, :], -jnp.inf))
    kb = k * beta[..., None]
    a = jnp.where(strict, jnp.einsum("bhnid,bhnjd->bhnij", kb, k) * decay, 0.0)
    eye = jnp.eye(CHUNK, dtype=f32)
    t_inv = lax.linalg.triangular_solve(eye + a, jnp.broadcast_to(eye, a.shape),
                                        left_side=True, lower=True, unit_diagonal=True)
    u = t_inv @ (v * beta[..., None])
    w = t_inv @ (kb * jnp.exp(G)[..., None])
    qk = jnp.einsum("bhnid,bhnjd->bhnij", q, k) * decay
    qg = q * jnp.exp(G)[..., None]
    kd = k * jnp.exp(G[..., -1:] - G)[..., None]
    gl = jnp.exp(G[..., -1])

    def step(S, inp):
        u_n, w_n, qk_n, qg_n, kd_n, gl_n = inp
        v_new = u_n - w_n @ S
        o = qg_n @ S + qk_n @ v_new
        S = S * gl_n[..., None, None] + jnp.einsum("bhcd,bhce->bhde", kd_n, v_new)
        return S, o

    xs = tuple(jnp.moveaxis(t, 2, 0) for t in (u, w, qk, qg, kd, gl))
    S0 = jnp.zeros(q.shape[:2] + (q.shape[-1], v.shape[-1]), f32)
    _, o = lax.scan(step, S0, xs)
    return _from_chunks(jnp.moveaxis(o, 0, 2))


def ssd_chunked(x, dt, a_neg, b_mat, c_mat):
    f32 = jnp.float32
    x, b_mat, c_mat = (_to_chunks(t.astype(f32)) for t in (x, b_mat, c_mat))
    dt = _to_chunks(dt.astype(f32))
    acum = jnp.cumsum(dt * a_neg.astype(f32)[:, None, None], axis=-1)
    incl, _ = _chunk_masks()
    decay = jnp.exp(jnp.where(incl, acum[..., :, None] - acum[..., None, :], -jnp.inf))
    scores = jnp.einsum("bhnis,bhnjs->bhnij", c_mat, b_mat) * decay * dt[..., None, :]
    y_diag = scores @ x
    states = jnp.einsum("bhnjs,bhnjp->bhnsp",
                        b_mat * (jnp.exp(acum[..., -1:] - acum) * dt)[..., None], x)
    gl = jnp.exp(acum[..., -1])

    def step(h, inp):
        st, g_ = inp
        return h * g_[..., None, None] + st, h

    h0 = jnp.zeros(x.shape[:2] + (SSM_N, SSM_P), f32)
    _, h_prev = lax.scan(step, h0, (jnp.moveaxis(states, 2, 0), jnp.moveaxis(gl, 2, 0)))
    h_prev = jnp.moveaxis(h_prev, 0, 2)
    y_off = jnp.einsum("bhnis,bhnsp->bhnip", c_mat * jnp.exp(acum)[..., None], h_prev)
    return _from_chunks(y_diag + y_off)


def hybrid_mixer(h, w_in, gdn_conv_w, gdn_A_log, gdn_dt_bias, gdn_norm_w, ssm_conv_w,
                 ssm_conv_b, ssm_A_log, ssm_dt_bias, ssm_D, ssm_norm_w, w_out):
    f32 = jnp.float32
    b, l, _ = h.shape
    proj = h @ w_in
    gdn_qkv, gdn_z, gdn_b, gdn_a, ssm_xbc, ssm_z, ssm_dt = jnp.split(proj, IN_SPLITS, axis=-1)

    qkv = jax.nn.silu(causal_dwconv(gdn_qkv, gdn_conv_w))
    q, k, v = jnp.split(qkv, [GDN_QK, 2 * GDN_QK], axis=-1)
    q = l2norm(q.reshape(b, l, GDN_HEADS, GDN_DK))
    k = l2norm(k.reshape(b, l, GDN_HEADS, GDN_DK))
    v = v.reshape(b, l, GDN_HEADS, GDN_DV)
    g = -jnp.exp(gdn_A_log.astype(f32)) * jax.nn.softplus(gdn_a.astype(f32) + gdn_dt_bias)
    beta = jax.nn.sigmoid(gdn_b.astype(f32))
    o = gated_delta_rule(q, k, v, g, beta)
    o = rmsnorm(o, gdn_norm_w) * jax.nn.silu(gdn_z.reshape(b, l, GDN_HEADS, GDN_DV).astype(f32))
    o = o.reshape(b, l, GDN_V).astype(h.dtype)

    xbc = jax.nn.silu(causal_dwconv(ssm_xbc, ssm_conv_w, ssm_conv_b))
    xs, bm, cm = jnp.split(xbc, [SSM_INNER, SSM_INNER + SSM_BC], axis=-1)
    xs = xs.reshape(b, l, SSM_HEADS, SSM_P)
    rep = SSM_HEADS // SSM_GROUPS
    bm = jnp.repeat(bm.reshape(b, l, SSM_GROUPS, SSM_N), rep, axis=2)
    cm = jnp.repeat(cm.reshape(b, l, SSM_GROUPS, SSM_N), rep, axis=2)
    dt = jax.nn.softplus(ssm_dt.astype(f32) + ssm_dt_bias)
    y = ssd_chunked(xs, dt, -jnp.exp(ssm_A_log.astype(f32)), bm, cm)
    y = y + ssm_D.astype(f32)[:, None] * xs.astype(f32)
    gshape = (b, l, SSM_GROUPS, SSM_INNER // SSM_GROUPS)
    y = y.reshape(gshape) * jax.nn.silu(ssm_z.reshape(gshape).astype(f32))
    y = rmsnorm(y, ssm_norm_w.reshape(SSM_GROUPS, SSM_INNER // SSM_GROUPS))
    y = y.reshape(b, l, SSM_INNER).astype(h.dtype)

    return jnp.concatenate([o, y], axis=-1) @ w_out


def sqrelu_mlp(h, w1, w2):
    return jnp.square(jax.nn.relu(h @ w1)) @ w2


def setup_inputs(seed: int = 0) -> dict:
    key = jax.random.key(seed)
    ks = jax.random.split(key, 24)
    f32 = jnp.float32

    def nrm(k, shape, scale):
        return jax.random.normal(k, shape, f32) * scale

    def gain(k, shape):
        return 1.0 + 0.02 * jax.random.normal(k, shape, f32)

    def dt_bias_init(k, n):
        u = jax.random.uniform(k, (DEPTH, n), f32)
        dt = jnp.exp(u * (math.log(DT_MAX) - math.log(DT_MIN)) + math.log(DT_MIN))
        return dt + jnp.log(-jnp.expm1(-dt))

    def a_log_init(k, n):
        return jnp.log(jax.random.uniform(k, (DEPTH, n), f32, 1.0, 16.0))

    return {
        "x": nrm(ks[0], (BATCH, SEQ, D_MODEL), 1.0),
        "c": nrm(ks[1], (BATCH, D_MODEL), 1.0),
        "ln1_w": gain(ks[2], (DEPTH, D_MODEL)),
        "ln2_w": gain(ks[3], (DEPTH, D_MODEL)),
        "ada_w": nrm(ks[4], (DEPTH, D_MODEL, N_MOD * D_MODEL), 0.5 * D_MODEL ** -0.5),
        "ada_b": nrm(ks[5], (DEPTH, N_MOD * D_MODEL), 0.02),
        "w_in": nrm(ks[6], (DEPTH, D_MODEL, D_IN_PROJ), D_MODEL ** -0.5),
        "gdn_conv_w": nrm(ks[7], (DEPTH, CONV_K, GDN_CONV_CH), CONV_K ** -0.5),
        "gdn_A_log": a_log_init(ks[8], GDN_HEADS),
        "gdn_dt_bias": dt_bias_init(ks[9], GDN_HEADS),
        "gdn_norm_w": gain(ks[10], (DEPTH, GDN_DV)),
        "ssm_conv_w": nrm(ks[11], (DEPTH, CONV_K, SSM_CONV_CH), CONV_K ** -0.5),
        "ssm_conv_b": nrm(ks[12], (DEPTH, SSM_CONV_CH), 0.02),
        "ssm_A_log": a_log_init(ks[13], SSM_HEADS),
        "ssm_dt_bias": dt_bias_init(ks[14], SSM_HEADS),
        "ssm_D": gain(ks[15], (DEPTH, SSM_HEADS)),
        "ssm_norm_w": gain(ks[16], (DEPTH, SSM_INNER)),
        "w_out": nrm(ks[17], (DEPTH, D_MIX, D_MODEL), D_MIX ** -0.5),
        "w_ff1": nrm(ks[18], (DEPTH, D_MODEL, D_FF), D_MODEL ** -0.5),
        "w_ff2": nrm(ks[19], (DEPTH, D_FF, D_MODEL), D_FF ** -0.5),
        "final_norm_w": gain(ks[20], (D_MODEL,)),
    }


def reference(x, c, ln1_w, ln2_w, ada_w, ada_b, w_in, gdn_conv_w, gdn_A_log, gdn_dt_bias,
              gdn_norm_w, ssm_conv_w, ssm_conv_b, ssm_A_log, ssm_dt_bias, ssm_D, ssm_norm_w,
              w_out, w_ff1, w_ff2, final_norm_w):
    b = x.shape[0]
    c_act = jax.nn.silu(c)
    for layer in range(DEPTH):
        mod = (c_act @ ada_w[layer] + ada_b[layer]).reshape(b, N_MOD, D_MODEL)[:, :, None, :]
        shift1, scale1, gate1, shift2, scale2, gate2 = (mod[:, i] for i in range(N_MOD))
        h = rmsnorm(x, ln1_w[layer]) * (1.0 + scale1) + shift1
        x = x + gate1 * hybrid_mixer(
            h, w_in[layer], gdn_conv_w[layer], gdn_A_log[layer], gdn_dt_bias[layer],
            gdn_norm_w[layer], ssm_conv_w[layer], ssm_conv_b[layer], ssm_A_log[layer],
            ssm_dt_bias[layer], ssm_D[layer], ssm_norm_w[layer], w_out[layer])
        h = rmsnorm(x, ln2_w[layer]) * (1.0 + scale2) + shift2
        x = x + gate2 * sqrelu_mlp(h, w_ff1[layer], w_ff2[layer])
    return rmsnorm(x, final_norm_w)
```

```python
import functools
import math

import jax
import jax.numpy as jnp
from jax import lax
from jax.experimental import pallas as pl
from jax.experimental.pallas import tpu as pltpu

F32 = jnp.float32
BF16 = jnp.bfloat16
HIGHEST = lax.Precision.HIGHEST

EPS = 1e-6
D_MODEL = 1024
N_MOD = 6
CHUNK = 64
CONV_K = 4
HEADS = 8
HEAD_W = 64
GDN_QK = 512
GDN_V = 512
SSM_INNER = 512
SSM_BC = 128
SSM_GROUP_W = 256
D_MIX = GDN_V + SSM_INNER
D_FF = 4 * D_MODEL
GDN_CONV_CH = 2 * GDN_QK + GDN_V
SSM_CONV_CH = SSM_INNER + 2 * SSM_BC
CONV_CH = GDN_CONV_CH + SSM_CONV_CH
GATE_W = 128
PROJ_W = CONV_CH + GDN_V + SSM_INNER + GATE_W
EXP_W = 4 * 512

LANE = 128
HALO = 8
VMEM_LIMIT = 56 * 1024 * 1024

TM_PROJ = 256
TM_MIX = 256
TM_OUT = 256
FF_BLOCK = 1024


def _mm(a, b):
  return jnp.dot(a.astype(BF16), b.astype(BF16), preferred_element_type=F32)


def _mm_nt(a, b):
  return lax.dot_general(a.astype(BF16), b.astype(BF16), (((1,), (1,)), ((), ())),
                         preferred_element_type=F32)


def _mm_tn(a, b):
  return lax.dot_general(a.astype(BF16), b.astype(BF16), (((0,), (0,)), ((), ())),
                         preferred_element_type=F32)


def _split3(x):
  x1 = x.astype(BF16)
  r = x - x1.astype(F32)
  x2 = r.astype(BF16)
  x3 = (r - x2.astype(F32)).astype(BF16)
  return x1, x2, x3


def _split2(x):
  x1 = x.astype(BF16)
  x2 = (x - x1.astype(F32)).astype(BF16)
  return x1, x2


def _group_sum(x, ones2):
  outs = []
  for c in range(x.shape[1] // LANE):
    hi, lo = _split2(x[:, c * LANE:(c + 1) * LANE])
    outs.append(jnp.dot(jnp.concatenate([hi, lo], axis=1), ones2,
                        preferred_element_type=F32))
  return outs[0] if len(outs) == 1 else jnp.concatenate(outs, axis=1)


def _silu(x):
  return x * jax.nn.sigmoid(x)


def _softplus(x):
  return jnp.maximum(x, 0.0) + jnp.log1p(jnp.exp(-jnp.abs(x)))


def _mod_kernel(c_ref, w_ref, b_ref, o_ref):
  c = c_ref[...]
  o_ref[...] = jnp.dot(_silu(c), w_ref[...], precision=HIGHEST,
                       preferred_element_type=F32) + b_ref[...]


def _ada_mod(c, ada_w, ada_b):
  b = c.shape[0]
  n = ada_w.shape[1]
  blk = D_MODEL
  return pl.pallas_call(
      _mod_kernel,
      grid=(n // blk,),
      in_specs=[
          pl.BlockSpec((b, D_MODEL), lambda j: (0, 0)),
          pl.BlockSpec((D_MODEL, blk), lambda j: (0, j)),
          pl.BlockSpec((1, blk), lambda j: (0, j)),
      ],
      out_specs=pl.BlockSpec((b, blk), lambda j: (0, j)),
      out_shape=jax.ShapeDtypeStruct((b, n), F32),
      compiler_params=pltpu.CompilerParams(
          dimension_semantics=("arbitrary",), vmem_limit_bytes=VMEM_LIMIT),
      name="ada_mod",
  )(c, ada_w, ada_b.reshape(1, n))


def _proj_kernel(x_ref, mod_ref, ln_ref, w_ref, cw_ref, cb_ref, ones_ref,
                 qkv_ref, xbc_ref, zz_ref, gate_ref, cbuf):
  tm = x_ref.shape[1]

  @pl.when(pl.program_id(1) == 0)
  def _():
    cbuf[0:HALO, :] = jnp.zeros((HALO, CONV_CH), F32)

  x = x_ref[0]
  mod = mod_ref[0]
  ms = jnp.mean(x * x, axis=-1, keepdims=True)
  h = x * lax.rsqrt(ms + EPS) * (ln_ref[...] * (1.0 + mod[1:2])) + mod[0:1]
  proj = jnp.dot(h.astype(BF16), w_ref[...], preferred_element_type=F32)

  cbuf[HALO:HALO + tm, :] = proj[:, :CONV_CH]
  cw = cw_ref[...]
  conv = cb_ref[...] + cw[0:1] * cbuf[HALO - 3:HALO - 3 + tm, :]
  for j in range(1, CONV_K):
    conv = conv + cw[j:j + 1] * cbuf[HALO - 3 + j:HALO - 3 + j + tm, :]
  cbuf[0:HALO, :] = cbuf[tm:tm + HALO, :]
  act = _silu(conv)

  ones2 = ones_ref[...]
  for s in range(2):
    u = act[:, s * GDN_QK:(s + 1) * GDN_QK]
    qkv_ref[0, :, s * GDN_QK:(s + 1) * GDN_QK] = u * lax.rsqrt(_group_sum(u * u, ones2) + EPS)
  qkv_ref[0, :, 2 * GDN_QK:GDN_CONV_CH] = act[:, 2 * GDN_QK:GDN_CONV_CH]
  xbc_ref[0] = act[:, GDN_CONV_CH:CONV_CH]
  zz_ref[0] = proj[:, CONV_CH:CONV_CH + D_MIX]
  gate_ref[0] = proj[:, CONV_CH + D_MIX:PROJ_W]


def _in_proj(x, mod, ln1_w, w_in_p, conv_w, conv_b, ones64x2):
  b, l, _ = x.shape
  tm = TM_PROJ
  const = lambda i, t: (0, 0)
  row = lambda i, t: (i, t, 0)
  return pl.pallas_call(
      _proj_kernel,
      grid=(b, l // tm),
      in_specs=[
          pl.BlockSpec((1, tm, D_MODEL), row),
          pl.BlockSpec((1, N_MOD, D_MODEL), lambda i, t: (i, 0, 0)),
          pl.BlockSpec((1, D_MODEL), const),
          pl.BlockSpec((D_MODEL, PROJ_W), const),
          pl.BlockSpec((CONV_K, CONV_CH), const),
          pl.BlockSpec((1, CONV_CH), const),
          pl.BlockSpec((2 * LANE, LANE), const),
      ],
      out_specs=[
          pl.BlockSpec((1, tm, GDN_CONV_CH), row),
          pl.BlockSpec((1, tm, SSM_CONV_CH), row),
          pl.BlockSpec((1, tm, D_MIX), row),
          pl.BlockSpec((1, tm, GATE_W), row),
      ],
      out_shape=[
          jax.ShapeDtypeStruct((b, l, GDN_CONV_CH), F32),
          jax.ShapeDtypeStruct((b, l, SSM_CONV_CH), F32),
          jax.ShapeDtypeStruct((b, l, D_MIX), F32),
          jax.ShapeDtypeStruct((b, l, GATE_W), F32),
      ],
      scratch_shapes=[pltpu.VMEM((tm + HALO, CONV_CH), F32)],
      compiler_params=pltpu.CompilerParams(
          dimension_semantics=("arbitrary", "arbitrary"), vmem_limit_bytes=VMEM_LIMIT),
      name="in_proj",
  )(x, mod, ln1_w, w_in_p, conv_w, conv_b, ones64x2)


def _block_diag(y, lo):
  return jnp.concatenate([jnp.where(lo, y, 0.0), jnp.where(lo, 0.0, y)], axis=0)


def _mixer_kernel(qkv_ref, xbc_ref, zz_ref, gate_ref, gbias_ref, alog_ref, gnw_ref, snw_ref,
                  sd_ref, lt_ref, e3_ref, ones64_ref, ones128_ref,
                  out_ref, sb_ref, hs_ref, ex_ref):
  tm = qkv_ref.shape[1]
  n_chunks = tm // CHUNK
  n_pairs = HEADS // 2

  @pl.when(pl.program_id(1) == 0)
  def _():
    sb_ref[...] = jnp.zeros(sb_ref.shape, F32)
    hs_ref[...] = jnp.zeros(hs_ref.shape, F32)

  raw = gate_ref[0]
  slot = lax.broadcasted_iota(jnp.int32, (tm, GATE_W), 1) >> 3
  slot1 = lax.broadcasted_iota(jnp.int32, (1, GATE_W), 1) >> 3
  neg_a = -jnp.exp(alog_ref[...])
  gmult = jnp.where(slot1 == 1, neg_a, jnp.where(slot1 == 3, neg_a, jnp.where(slot1 == 2, 1.0, 0.0)))
  pre = jnp.where(slot == 0, jax.nn.sigmoid(raw), _softplus(raw + gbias_ref[...]) * gmult)
  cs = jnp.dot(lt_ref[...], pre, precision=HIGHEST, preferred_element_type=F32)
  comb = jnp.where(slot == 1, cs, jnp.where(slot == 3, cs, pre))
  ex_ref[...] = jnp.dot(jnp.concatenate(_split3(comb), axis=1), e3_ref[...],
                        preferred_element_type=F32)

  row = lax.broadcasted_iota(jnp.int32, (CHUNK, LANE), 0)
  lane = lax.broadcasted_iota(jnp.int32, (CHUNK, LANE), 1)
  pos = lane & (HEAD_W - 1)
  lo = lane < HEAD_W
  eye2 = jnp.where(pos == row, 1.0, 0.0)
  strict2 = pos < row
  row8 = lax.broadcasted_iota(jnp.int32, (CHUNK, 512), 0)
  pos8 = lax.broadcasted_iota(jnp.int32, (CHUNK, 512), 1) & (HEAD_W - 1)
  incl8 = pos8 <= row8
  eye8 = pos8 == row8
  rr = lax.broadcasted_iota(jnp.int32, (LANE, LANE), 0)
  ll = lax.broadcasted_iota(jnp.int32, (LANE, LANE), 1)
  bd_mask = (rr >> 6) == (ll >> 6)
  rg = lax.broadcasted_iota(jnp.int32, (SSM_BC, SSM_INNER), 0)
  lg = lax.broadcasted_iota(jnp.int32, (SSM_BC, SSM_INNER), 1)
  grp_mask = (rg >> 6) == (lg >> 8)
  ones3 = jnp.ones((CHUNK, 3 * CHUNK), BF16)
  scale = HEAD_W ** -0.5

  def row_bcast(col_bc):
    d = jnp.where(eye8, col_bc, 0.0)
    return jnp.dot(ones3, jnp.concatenate(_split3(d), axis=0), preferred_element_type=F32)

  def chunk_body(c, carry):
    r0 = pl.multiple_of(c * CHUNK, CHUNK)
    rows = pl.ds(r0, CHUNK)
    beta_bc = ex_ref[rows, 0:512]
    g_bc = ex_ref[rows, 512:1024]
    dt_bc = ex_ref[rows, 1024:1536]
    ac_bc = ex_ref[rows, 1536:2048]

    dec = jnp.exp(jnp.where(incl8, g_bc - row_bcast(g_bc), -jnp.inf))
    exp_g = jnp.exp(g_bc)
    g_last = g_bc[CHUNK - 1:CHUNK, :]
    exp_gl_g = jnp.exp(g_last - g_bc)
    gl_row = jnp.exp(g_last)
    o_parts = []
    for p in range(n_pairs):
      cols = slice(p * LANE, (p + 1) * LANE)
      q2 = qkv_ref[0, rows, p * LANE:(p + 1) * LANE]
      k2 = qkv_ref[0, rows, GDN_QK + p * LANE:GDN_QK + (p + 1) * LANE]
      v2 = qkv_ref[0, rows, 2 * GDN_QK + p * LANE:2 * GDN_QK + (p + 1) * LANE]
      beta2 = beta_bc[:, cols]
      eg2 = exp_g[:, cols]
      d2 = dec[:, cols]
      kb2 = k2 * beta2
      aq = _mm_nt(jnp.concatenate([kb2, q2 * scale], axis=0), _block_diag(k2, lo))
      n1 = jnp.where(strict2, -(aq[0:CHUNK] * d2), 0.0)
      qkd2 = aq[CHUNK:2 * CHUNK] * d2
      xk = _mm(n1, _block_diag(n1, lo))
      pk = eye2 + n1
      for _ in range(4):
        both = _mm(xk, jnp.concatenate([_block_diag(xk, lo), _block_diag(pk, lo)], axis=1))
        xk = both[:, 0:LANE]
        pk = pk + both[:, LANE:2 * LANE]
      t2 = pk + _mm(xk, _block_diag(pk, lo))
      uw = _mm(t2, jnp.concatenate([_block_diag(v2 * beta2, lo), _block_diag(kb2 * eg2, lo)], axis=1))
      u2 = uw[:, 0:LANE]
      w2 = uw[:, LANE:2 * LANE]
      qg2 = q2 * scale * eg2
      kd2 = k2 * exp_gl_g[:, cols]
      sb = sb_ref[p]
      wq = _mm(jnp.concatenate([w2, qg2], axis=0), sb)
      vn = u2 - wq[0:CHUNK]
      o_parts.append(wq[CHUNK:2 * CHUNK] + _mm(qkd2, _block_diag(vn, lo)))
      sb_ref[p] = sb * gl_row[:, cols] + jnp.where(bd_mask, _mm_tn(kd2, vn), 0.0)
    o = jnp.concatenate(o_parts, axis=1)
    ms = _group_sum(o * o, ones64_ref[...]) * (1.0 / HEAD_W)
    z = zz_ref[0, rows, 0:GDN_V]
    out_ref[0, rows, 0:GDN_V] = (o * lax.rsqrt(ms + EPS) * gnw_ref[...] * _silu(z)).astype(out_ref.dtype)

    xs = xbc_ref[0, rows, 0:SSM_INNER]
    bm = xbc_ref[0, rows, SSM_INNER:SSM_INNER + SSM_BC]
    cm = xbc_ref[0, rows, SSM_INNER + SSM_BC:SSM_CONV_CH]
    ldec = jnp.exp(jnp.where(incl8, ac_bc - row_bcast(ac_bc), -jnp.inf))
    bm_lo = jnp.where(lo, bm, 0.0)
    bm_hi = jnp.where(lo, 0.0, bm)
    r8 = jnp.concatenate([bm_lo] * (HEADS // 2) + [bm_hi] * (HEADS // 2), axis=0)
    sc = _mm_nt(cm, r8) * ldec
    xdt = xs * dt_bc
    yd_parts = []
    for p in range(n_pairs):
      cols = slice(p * LANE, (p + 1) * LANE)
      yd_parts.append(_mm(sc[:, cols], _block_diag(xdt[:, cols], lo)))
    y = jnp.concatenate(yd_parts, axis=1)
    a_last = ac_bc[CHUNK - 1:CHUNK, :]
    xw = xdt * jnp.exp(a_last - ac_bc)
    hs = hs_ref[...]
    y = y + _mm(cm, hs) * jnp.exp(ac_bc) + sd_ref[...] * xs
    hs_ref[...] = hs * jnp.exp(a_last) + jnp.where(grp_mask, _mm_tn(bm, xw), 0.0)
    yz = y * _silu(zz_ref[0, rows, GDN_V:D_MIX])
    sq = yz * yz
    ss = jnp.concatenate(
        [_group_sum(sq[:, g * SSM_GROUP_W:g * SSM_GROUP_W + LANE]
                    + sq[:, g * SSM_GROUP_W + LANE:(g + 1) * SSM_GROUP_W], ones128_ref[...])
         for g in range(2)], axis=1)
    ms = jnp.concatenate([ss[:, 0:LANE], ss[:, 0:LANE], ss[:, LANE:], ss[:, LANE:]], axis=1)
    ms = ms * (1.0 / SSM_GROUP_W)
    out_ref[0, rows, GDN_V:D_MIX] = (yz * lax.rsqrt(ms + EPS) * snw_ref[...]).astype(out_ref.dtype)
    return carry

  lax.fori_loop(0, n_chunks, chunk_body, 0)


def _mixer(qkv, xbc, zz, gates, gbias, alog, gnw_row, snw_row, sd_row, lt_bd, e3, ones64x2, ones128x2):
  b, l, _ = qkv.shape
  tm = TM_MIX
  const = lambda i, t: (0, 0)
  row = lambda i, t: (i, t, 0)
  return pl.pallas_call(
      _mixer_kernel,
      grid=(b, l // tm),
      in_specs=[
          pl.BlockSpec((1, tm, GDN_CONV_CH), row),
          pl.BlockSpec((1, tm, SSM_CONV_CH), row),
          pl.BlockSpec((1, tm, D_MIX), row),
          pl.BlockSpec((1, tm, GATE_W), row),
          pl.BlockSpec((1, GATE_W), const),
          pl.BlockSpec((1, GATE_W), const),
          pl.BlockSpec((1, GDN_V), const),
          pl.BlockSpec((1, SSM_INNER), const),
          pl.BlockSpec((1, SSM_INNER), const),
          pl.BlockSpec((tm, tm), const),
          pl.BlockSpec((3 * GATE_W, EXP_W), const),
          pl.BlockSpec((2 * LANE, LANE), const),
          pl.BlockSpec((2 * LANE, LANE), const),
      ],
      out_specs=pl.BlockSpec((1, tm, D_MIX), row),
      out_shape=jax.ShapeDtypeStruct((b, l, D_MIX), BF16),
      scratch_shapes=[
          pltpu.VMEM((HEADS // 2, LANE, LANE), F32),
          pltpu.VMEM((SSM_BC, SSM_INNER), F32),
          pltpu.VMEM((tm, EXP_W), F32),
      ],
      compiler_params=pltpu.CompilerParams(
          dimension_semantics=("arbitrary", "arbitrary"), vmem_limit_bytes=VMEM_LIMIT),
      name="mixer",
  )(qkv, xbc, zz, gates, gbias, alog, gnw_row, snw_row, sd_row, lt_bd, e3, ones64x2, ones128x2)


def _out_kernel(x_ref, mix_ref, mod_ref, ln_ref, fnw_ref, wo_ref, w1_ref, w2_ref, o_ref, *, final_norm):
  x = x_ref[0]
  mod = mod_ref[0]
  x1 = x + mod[2:3] * jnp.dot(mix_ref[0], wo_ref[...], preferred_element_type=F32)
  ms = jnp.mean(x1 * x1, axis=-1, keepdims=True)
  h = (x1 * lax.rsqrt(ms + EPS) * (ln_ref[...] * (1.0 + mod[4:5])) + mod[3:4]).astype(BF16)
  acc = jnp.zeros(x.shape, F32)
  for j in range(D_FF // FF_BLOCK):
    a = jnp.maximum(jnp.dot(h, w1_ref[:, j * FF_BLOCK:(j + 1) * FF_BLOCK],
                            preferred_element_type=F32), 0.0)
    acc = acc + jnp.dot((a * a).astype(BF16), w2_ref[j * FF_BLOCK:(j + 1) * FF_BLOCK, :],
                        preferred_element_type=F32)
  x2 = x1 + mod[5:6] * acc
  if final_norm:
    ms2 = jnp.mean(x2 * x2, axis=-1, keepdims=True)
    x2 = x2 * lax.rsqrt(ms2 + EPS) * fnw_ref[...]
  o_ref[0] = x2


def _out_mlp(x, mix, mod, ln2_w, fnw, wo, w1, w2, *, final_norm):
  b, l, _ = x.shape
  tm = TM_OUT
  const = lambda i, t: (0, 0)
  row = lambda i, t: (i, t, 0)
  return pl.pallas_call(
      functools.partial(_out_kernel, final_norm=final_norm),
      grid=(b, l // tm),
      in_specs=[
          pl.BlockSpec((1, tm, D_MODEL), row),
          pl.BlockSpec((1, tm, D_MIX), row),
          pl.BlockSpec((1, N_MOD, D_MODEL), lambda i, t: (i, 0, 0)),
          pl.BlockSpec((1, D_MODEL), const),
          pl.BlockSpec((1, D_MODEL), const),
          pl.BlockSpec((D_MIX, D_MODEL), const),
          pl.BlockSpec((D_MODEL, D_FF), const),
          pl.BlockSpec((D_FF, D_MODEL), const),
      ],
      out_specs=pl.BlockSpec((1, tm, D_MODEL), row),
      out_shape=jax.ShapeDtypeStruct((b, l, D_MODEL), F32),
      compiler_params=pltpu.CompilerParams(
          dimension_semantics=("arbitrary", "arbitrary"), vmem_limit_bytes=VMEM_LIMIT),
      name="out_mlp",
  )(x, mix, mod, ln2_w, fnw, wo, w1, w2)


def _expand_matrix():
  src = jnp.arange(GATE_W)[:, None]
  dst = jnp.arange(EXP_W)[None, :]
  e = (src == (dst // 512) * HEADS + (dst % 512) // HEAD_W).astype(BF16)
  return jnp.concatenate([e, e, e], axis=0)


def _group_ones(width):
  i = jnp.arange(LANE)
  m = ((i[:, None] // width) == (i[None, :] // width)).astype(BF16)
  return jnp.concatenate([m, m], axis=0)


def _chunk_tril(tm):
  i = jnp.arange(tm)
  return ((i[:, None] >= i[None, :]) & ((i[:, None] // CHUNK) == (i[None, :] // CHUNK))).astype(F32)


def kernel(x, c, ln1_w, ln2_w, ada_w, ada_b, w_in, gdn_conv_w, gdn_A_log, gdn_dt_bias, gdn_norm_w,
           ssm_conv_w, ssm_conv_b, ssm_A_log, ssm_dt_bias, ssm_D, ssm_norm_w, w_out, w_ff1, w_ff2,
           final_norm_w):
  depth = ln1_w.shape[0]
  b = x.shape[0]
  ones64x2 = _group_ones(HEAD_W)
  ones128x2 = _group_ones(LANE)
  e3 = _expand_matrix()
  lt_bd = _chunk_tril(TM_MIX)
  c_qkv, c_z, c_b, c_a, c_xbc, c_sz, c_dt = (
      GDN_CONV_CH, GDN_V, HEADS, HEADS, SSM_CONV_CH, SSM_INNER, HEADS)
  o_z = c_qkv
  o_b = o_z + c_z
  o_a = o_b + c_b
  o_xbc = o_a + c_a
  o_sz = o_xbc + c_xbc
  o_dt = o_sz + c_sz
  for layer in range(depth):
    mod = _ada_mod(c, ada_w[layer], ada_b[layer]).reshape(b, N_MOD, D_MODEL)
    w = w_in[layer]
    w_dt = w[:, o_dt:o_dt + c_dt]
    w_in_p = jnp.concatenate(
        [w[:, 0:c_qkv], w[:, o_xbc:o_xbc + c_xbc], w[:, o_z:o_z + c_z], w[:, o_sz:o_sz + c_sz],
         w[:, o_b:o_b + c_b], w[:, o_a:o_a + c_a], w_dt, w_dt,
         jnp.zeros((D_MODEL, GATE_W - 4 * HEADS), F32)], axis=1).astype(BF16)
    conv_w = jnp.concatenate([gdn_conv_w[layer], ssm_conv_w[layer]], axis=1)
    conv_b = jnp.concatenate([jnp.zeros((GDN_CONV_CH,), F32), ssm_conv_b[layer]])[None, :]
    zeros8 = jnp.zeros((HEADS,), F32)
    pad = jnp.zeros((GATE_W - 4 * HEADS,), F32)
    gbias = jnp.concatenate([zeros8, gdn_dt_bias[layer], ssm_dt_bias[layer], ssm_dt_bias[layer], pad])[None, :]
    alog = jnp.concatenate([zeros8, gdn_A_log[layer], zeros8, ssm_A_log[layer], pad])[None, :]
    gnw_row = jnp.tile(gdn_norm_w[layer], HEADS)[None, :]
    snw_row = ssm_norm_w[layer][None, :]
    sd_row = jnp.repeat(ssm_D[layer], HEAD_W)[None, :]

    qkv, xbc, zz, gates = _in_proj(x, mod, ln1_w[layer][None, :], w_in_p, conv_w, conv_b, ones64x2)
    mix = _mixer(qkv, xbc, zz, gates, gbias, alog, gnw_row, snw_row, sd_row, lt_bd, e3,
                 ones64x2, ones128x2)
    x = _out_mlp(x, mix, mod, ln2_w[layer][None, :], final_norm_w[None, :],
                 w_out[layer].astype(BF16), w_ff1[layer].astype(BF16), w_ff2[layer].astype(BF16),
                 final_norm=layer == depth - 1)
  return x
```

```python
import functools
import math

import jax
import jax.numpy as jnp
from jax import lax
from jax.experimental import pallas as pl
from jax.experimental.pallas import tpu as pltpu

F32 = jnp.float32
BF16 = jnp.bfloat16
HIGHEST = lax.Precision.HIGHEST

EPS = 1e-6
D_MODEL = 1024
N_MOD = 6
CHUNK = 64
CONV_K = 4
HEADS = 8
HEAD_W = 64
GDN_QK = 512
GDN_V = 512
SSM_INNER = 512
SSM_BC = 128
SSM_GROUP_W = 256
D_MIX = GDN_V + SSM_INNER
D_FF = 4 * D_MODEL
GDN_CONV_CH = 2 * GDN_QK + GDN_V
SSM_CONV_CH = SSM_INNER + 2 * SSM_BC
CONV_CH = GDN_CONV_CH + SSM_CONV_CH
GATE_W = 128
PROJ_W = CONV_CH + GDN_V + SSM_INNER + GATE_W
EXP_W = 4 * 512

LANE = 128
HALO = 8
VMEM_LIMIT = 56 * 1024 * 1024

TM_PROJ = 256
TM_MIX = 256
CHUNKS_PER_STEP = 2
TM_OUT = 256
FF_BLOCK = 1024


def _mm(a, b):
  return jnp.dot(a.astype(BF16), b.astype(BF16), preferred_element_type=F32)


def _mm_nt(a, b):
  return lax.dot_general(a.astype(BF16), b.astype(BF16), (((1,), (1,)), ((), ())),
                         preferred_element_type=F32)


def _mm_tn(a, b):
  return lax.dot_general(a.astype(BF16), b.astype(BF16), (((0,), (0,)), ((), ())),
                         preferred_element_type=F32)


def _split3(x):
  x1 = x.astype(BF16)
  r = x - x1.astype(F32)
  x2 = r.astype(BF16)
  x3 = (r - x2.astype(F32)).astype(BF16)
  return x1, x2, x3


def _split2(x):
  x1 = x.astype(BF16)
  x2 = (x - x1.astype(F32)).astype(BF16)
  return x1, x2


def _group_sum(x, ones2):
  outs = []
  for c in range(x.shape[1] // LANE):
    hi, lo = _split2(x[:, c * LANE:(c + 1) * LANE])
    outs.append(jnp.dot(jnp.concatenate([hi, lo], axis=1), ones2,
                        preferred_element_type=F32))
  return outs[0] if len(outs) == 1 else jnp.concatenate(outs, axis=1)


def _silu(x):
  return x * jax.nn.sigmoid(x)


def _softplus(x):
  return jnp.maximum(x, 0.0) + jnp.log1p(jnp.exp(-jnp.abs(x)))


def _mod_kernel(c_ref, w_ref, b_ref, o_ref):
  c = c_ref[...]
  o_ref[...] = jnp.dot(_silu(c), w_ref[...], precision=HIGHEST,
                       preferred_element_type=F32) + b_ref[...]


def _ada_mod(c, ada_w, ada_b):
  b = c.shape[0]
  n = ada_w.shape[1]
  blk = D_MODEL
  return pl.pallas_call(
      _mod_kernel,
      grid=(n // blk,),
      in_specs=[
          pl.BlockSpec((b, D_MODEL), lambda j: (0, 0)),
          pl.BlockSpec((D_MODEL, blk), lambda j: (0, j)),
          pl.BlockSpec((1, blk), lambda j: (0, j)),
      ],
      out_specs=pl.BlockSpec((b, blk), lambda j: (0, j)),
      out_shape=jax.ShapeDtypeStruct((b, n), F32),
      compiler_params=pltpu.CompilerParams(
          dimension_semantics=("arbitrary",), vmem_limit_bytes=VMEM_LIMIT),
      name="ada_mod",
  )(c, ada_w, ada_b.reshape(1, n))


def _proj_kernel(x_ref, mod_ref, ln_ref, w_ref, cw_ref, cb_ref, ones_ref,
                 qkv_ref, xbc_ref, zz_ref, gate_ref, cbuf):
  tm = x_ref.shape[1]

  @pl.when(pl.program_id(1) == 0)
  def _():
    cbuf[0:HALO, :] = jnp.zeros((HALO, CONV_CH), F32)

  x = x_ref[0]
  mod = mod_ref[0]
  ms = jnp.mean(x * x, axis=-1, keepdims=True)
  h = x * lax.rsqrt(ms + EPS) * (ln_ref[...] * (1.0 + mod[1:2])) + mod[0:1]
  proj = jnp.dot(h.astype(BF16), w_ref[...], preferred_element_type=F32)

  cbuf[HALO:HALO + tm, :] = proj[:, :CONV_CH]
  cw = cw_ref[...]
  conv = cb_ref[...] + cw[0:1] * cbuf[HALO - 3:HALO - 3 + tm, :]
  for j in range(1, CONV_K):
    conv = conv + cw[j:j + 1] * cbuf[HALO - 3 + j:HALO - 3 + j + tm, :]
  cbuf[0:HALO, :] = cbuf[tm:tm + HALO, :]
  act = _silu(conv)

  ones2 = ones_ref[...]
  for s in range(2):
    u = act[:, s * GDN_QK:(s + 1) * GDN_QK]
    qkv_ref[0, :, s * GDN_QK:(s + 1) * GDN_QK] = u * lax.rsqrt(_group_sum(u * u, ones2) + EPS)
  qkv_ref[0, :, 2 * GDN_QK:GDN_CONV_CH] = act[:, 2 * GDN_QK:GDN_CONV_CH]
  xbc_ref[0] = act[:, GDN_CONV_CH:CONV_CH]
  zz_ref[0] = proj[:, CONV_CH:CONV_CH + D_MIX]
  gate_ref[0] = proj[:, CONV_CH + D_MIX:PROJ_W]


def _in_proj(x, mod, ln1_w, w_in_p, conv_w, conv_b, ones64x2):
  b, l, _ = x.shape
  tm = TM_PROJ
  const = lambda i, t: (0, 0)
  row = lambda i, t: (i, t, 0)
  return pl.pallas_call(
      _proj_kernel,
      grid=(b, l // tm),
      in_specs=[
          pl.BlockSpec((1, tm, D_MODEL), row),
          pl.BlockSpec((1, N_MOD, D_MODEL), lambda i, t: (i, 0, 0)),
          pl.BlockSpec((1, D_MODEL), const),
          pl.BlockSpec((D_MODEL, PROJ_W), const),
          pl.BlockSpec((CONV_K, CONV_CH), const),
          pl.BlockSpec((1, CONV_CH), const),
          pl.BlockSpec((2 * LANE, LANE), const),
      ],
      out_specs=[
          pl.BlockSpec((1, tm, GDN_CONV_CH), row),
          pl.BlockSpec((1, tm, SSM_CONV_CH), row),
          pl.BlockSpec((1, tm, D_MIX), row),
          pl.BlockSpec((1, tm, GATE_W), row),
      ],
      out_shape=[
          jax.ShapeDtypeStruct((b, l, GDN_CONV_CH), F32),
          jax.ShapeDtypeStruct((b, l, SSM_CONV_CH), F32),
          jax.ShapeDtypeStruct((b, l, D_MIX), F32),
          jax.ShapeDtypeStruct((b, l, GATE_W), F32),
      ],
      scratch_shapes=[pltpu.VMEM((tm + HALO, CONV_CH), F32)],
      compiler_params=pltpu.CompilerParams(
          dimension_semantics=("arbitrary", "arbitrary"), vmem_limit_bytes=VMEM_LIMIT),
      name="in_proj",
  )(x, mod, ln1_w, w_in_p, conv_w, conv_b, ones64x2)


def _block_diag(y, lo):
  return jnp.concatenate([jnp.where(lo, y, 0.0), jnp.where(lo, 0.0, y)], axis=0)


def _mixer_kernel(qkv_ref, xbc_ref, zz_ref, gate_ref, gbias_ref, alog_ref, gnw_ref, snw_ref,
                  sd_ref, lt_ref, e3_ref, ones64_ref, ones128_ref,
                  out_ref, sb_ref, hs_ref, ex_ref, u_ref, w_ref, qkd_ref, qg_ref, kd_ref, yd_ref,
                  xw_ref, o_ref, y_ref):
  tm = qkv_ref.shape[1]
  n_chunks = tm // CHUNK
  n_pairs = HEADS // 2

  @pl.when(pl.program_id(1) == 0)
  def _():
    sb_ref[...] = jnp.zeros(sb_ref.shape, F32)
    hs_ref[...] = jnp.zeros(hs_ref.shape, F32)

  raw = gate_ref[0]
  slot = lax.broadcasted_iota(jnp.int32, (tm, GATE_W), 1) >> 3
  slot1 = lax.broadcasted_iota(jnp.int32, (1, GATE_W), 1) >> 3
  neg_a = -jnp.exp(alog_ref[...])
  gmult = jnp.where(slot1 == 1, neg_a, jnp.where(slot1 == 3, neg_a, jnp.where(slot1 == 2, 1.0, 0.0)))
  pre = jnp.where(slot == 0, jax.nn.sigmoid(raw), _softplus(raw + gbias_ref[...]) * gmult)
  cs = jnp.dot(lt_ref[...], pre, precision=HIGHEST, preferred_element_type=F32)
  comb = jnp.where(slot == 1, cs, jnp.where(slot == 3, cs, pre))
  ex_ref[...] = jnp.dot(jnp.concatenate(_split3(comb), axis=1), e3_ref[...],
                        preferred_element_type=F32)

  row = lax.broadcasted_iota(jnp.int32, (CHUNK, LANE), 0)
  lane = lax.broadcasted_iota(jnp.int32, (CHUNK, LANE), 1)
  pos = lane & (HEAD_W - 1)
  lo = lane < HEAD_W
  eye2 = jnp.where(pos == row, 1.0, 0.0)
  strict2 = pos < row
  row8 = lax.broadcasted_iota(jnp.int32, (CHUNK, 512), 0)
  pos8 = lax.broadcasted_iota(jnp.int32, (CHUNK, 512), 1) & (HEAD_W - 1)
  incl8 = pos8 <= row8
  eye8 = pos8 == row8
  rr = lax.broadcasted_iota(jnp.int32, (LANE, LANE), 0)
  ll = lax.broadcasted_iota(jnp.int32, (LANE, LANE), 1)
  bd_mask = (rr >> 6) == (ll >> 6)
  rg = lax.broadcasted_iota(jnp.int32, (SSM_BC, SSM_INNER), 0)
  lg = lax.broadcasted_iota(jnp.int32, (SSM_BC, SSM_INNER), 1)
  grp_mask = (rg >> 6) == (lg >> 8)
  ones3 = jnp.ones((CHUNK, 3 * CHUNK), BF16)
  scale = HEAD_W ** -0.5

  def row_bcast(col_bc):
    d = jnp.where(eye8, col_bc, 0.0)
    return jnp.dot(ones3, jnp.concatenate(_split3(d), axis=0), preferred_element_type=F32)

  def chunk_rows(c):
    return pl.ds(pl.multiple_of(c * CHUNK, CHUNK), CHUNK)

  def intra(i, carry):
    chains = []
    for cc in range(CHUNKS_PER_STEP):
      rows = chunk_rows(i * CHUNKS_PER_STEP + cc)
      beta_bc = ex_ref[rows, 0:512]
      g_bc = ex_ref[rows, 512:1024]
      dec = jnp.exp(jnp.where(incl8, g_bc - row_bcast(g_bc), -jnp.inf))
      exp_g = jnp.exp(g_bc)
      exp_gl_g = jnp.exp(g_bc[CHUNK - 1:CHUNK, :] - g_bc)
      for p in range(n_pairs):
        cols = slice(p * LANE, (p + 1) * LANE)
        q2 = qkv_ref[0, rows, p * LANE:(p + 1) * LANE] * scale
        k2 = qkv_ref[0, rows, GDN_QK + p * LANE:GDN_QK + (p + 1) * LANE]
        v2 = qkv_ref[0, rows, 2 * GDN_QK + p * LANE:2 * GDN_QK + (p + 1) * LANE]
        kb2 = k2 * beta_bc[:, cols]
        qg_ref[rows, cols] = q2 * exp_g[:, cols]
        kd_ref[rows, cols] = k2 * exp_gl_g[:, cols]
        chains.append(dict(rows=rows, cols=cols, q2=q2, k2=k2, kb2=kb2, d2=dec[:, cols],
                           vb2=v2 * beta_bc[:, cols], kbg2=kb2 * exp_g[:, cols]))
    aq = [_mm_nt(jnp.concatenate([ch["kb2"], ch["q2"]], axis=0), _block_diag(ch["k2"], lo))
          for ch in chains]
    for ch, a in zip(chains, aq):
      qkd_ref[ch["rows"], ch["cols"]] = a[CHUNK:2 * CHUNK] * ch["d2"]
    n1 = [jnp.where(strict2, -(a[0:CHUNK] * ch["d2"]), 0.0) for ch, a in zip(chains, aq)]
    xk = [_mm(n, _block_diag(n, lo)) for n in n1]
    pk = [eye2 + n for n in n1]
    for _ in range(4):
      both = [_mm(x, jnp.concatenate([_block_diag(x, lo), _block_diag(p_, lo)], axis=1))
              for x, p_ in zip(xk, pk)]
      xk = [b_[:, 0:LANE] for b_ in both]
      pk = [p_ + b_[:, LANE:2 * LANE] for p_, b_ in zip(pk, both)]
    t2 = [p_ + _mm(x, _block_diag(p_, lo)) for x, p_ in zip(xk, pk)]
    for ch, t in zip(chains, t2):
      uw = _mm(t, jnp.concatenate([_block_diag(ch["vb2"], lo), _block_diag(ch["kbg2"], lo)], axis=1))
      u_ref[ch["rows"], ch["cols"]] = uw[:, 0:LANE]
      w_ref[ch["rows"], ch["cols"]] = uw[:, LANE:2 * LANE]

    for cc in range(CHUNKS_PER_STEP):
      rows = chunk_rows(i * CHUNKS_PER_STEP + cc)
      dt_bc = ex_ref[rows, 1024:1536]
      ac_bc = ex_ref[rows, 1536:2048]
      xs = xbc_ref[0, rows, 0:SSM_INNER]
      bm = xbc_ref[0, rows, SSM_INNER:SSM_INNER + SSM_BC]
      cm = xbc_ref[0, rows, SSM_INNER + SSM_BC:SSM_CONV_CH]
      ldec = jnp.exp(jnp.where(incl8, ac_bc - row_bcast(ac_bc), -jnp.inf))
      bm_lo = jnp.where(lo, bm, 0.0)
      bm_hi = jnp.where(lo, 0.0, bm)
      r8 = jnp.concatenate([bm_lo] * (HEADS // 2) + [bm_hi] * (HEADS // 2), axis=0)
      sc = _mm_nt(cm, r8) * ldec
      xdt = xs * dt_bc
      yd = [_mm(sc[:, p * LANE:(p + 1) * LANE], _block_diag(xdt[:, p * LANE:(p + 1) * LANE], lo))
            for p in range(n_pairs)]
      yd_ref[rows, :] = jnp.concatenate(yd, axis=1) + sd_ref[...] * xs
      xw_ref[rows, :] = xdt * jnp.exp(ac_bc[CHUNK - 1:CHUNK, :] - ac_bc)
    return carry

  lax.fori_loop(0, n_chunks // CHUNKS_PER_STEP, intra, 0)

  def recur(c, carry):
    rows = chunk_rows(c)
    last = pl.ds(pl.multiple_of(c * CHUNK, CHUNK) + CHUNK - 1, 1)
    gl_row = jnp.exp(ex_ref[last, 512:1024])
    sbs = [sb_ref[p] for p in range(n_pairs)]
    wq = [_mm(jnp.concatenate([w_ref[rows, p * LANE:(p + 1) * LANE],
                               qg_ref[rows, p * LANE:(p + 1) * LANE]], axis=0), sbs[p])
          for p in range(n_pairs)]
    vn = [u_ref[rows, p * LANE:(p + 1) * LANE] - wq[p][0:CHUNK] for p in range(n_pairs)]
    o = [wq[p][CHUNK:2 * CHUNK] + _mm(qkd_ref[rows, p * LANE:(p + 1) * LANE], _block_diag(vn[p], lo))
         for p in range(n_pairs)]
    for p in range(n_pairs):
      sb_ref[p] = (sbs[p] * gl_row[:, p * LANE:(p + 1) * LANE]
                   + jnp.where(bd_mask, _mm_tn(kd_ref[rows, p * LANE:(p + 1) * LANE], vn[p]), 0.0))
    o_ref[rows, :] = jnp.concatenate(o, axis=1)

    ac_bc = ex_ref[rows, 1536:2048]
    cm = xbc_ref[0, rows, SSM_INNER + SSM_BC:SSM_CONV_CH]
    bm = xbc_ref[0, rows, SSM_INNER:SSM_INNER + SSM_BC]
    hs = hs_ref[...]
    y_ref[rows, :] = yd_ref[rows, :] + _mm(cm, hs) * jnp.exp(ac_bc)
    hs_ref[...] = (hs * jnp.exp(ex_ref[last, 1536:2048])
                   + jnp.where(grp_mask, _mm_tn(bm, xw_ref[rows, :]), 0.0))
    return carry

  lax.fori_loop(0, n_chunks, recur, 0)

  o = o_ref[...]
  ms = _group_sum(o * o, ones64_ref[...]) * (1.0 / HEAD_W)
  out_ref[0, :, 0:GDN_V] = (o * lax.rsqrt(ms + EPS) * gnw_ref[...]
                            * _silu(zz_ref[0, :, 0:GDN_V])).astype(out_ref.dtype)
  yz = y_ref[...] * _silu(zz_ref[0, :, GDN_V:D_MIX])
  sq = yz * yz
  ss = [_group_sum(sq[:, g * SSM_GROUP_W:g * SSM_GROUP_W + LANE]
                   + sq[:, g * SSM_GROUP_W + LANE:(g + 1) * SSM_GROUP_W], ones128_ref[...])
        for g in range(2)]
  ms = jnp.concatenate([ss[0], ss[0], ss[1], ss[1]], axis=1) * (1.0 / SSM_GROUP_W)
  out_ref[0, :, GDN_V:D_MIX] = (yz * lax.rsqrt(ms + EPS) * snw_ref[...]).astype(out_ref.dtype)


def _mixer(qkv, xbc, zz, gates, gbias, alog, gnw_row, snw_row, sd_row, lt_bd, e3, ones64x2, ones128x2):
  b, l, _ = qkv.shape
  tm = TM_MIX
  const = lambda i, t: (0, 0)
  row = lambda i, t: (i, t, 0)
  return pl.pallas_call(
      _mixer_kernel,
      grid=(b, l // tm),
      in_specs=[
          pl.BlockSpec((1, tm, GDN_CONV_CH), row),
          pl.BlockSpec((1, tm, SSM_CONV_CH), row),
          pl.BlockSpec((1, tm, D_MIX), row),
          pl.BlockSpec((1, tm, GATE_W), row),
          pl.BlockSpec((1, GATE_W), const),
          pl.BlockSpec((1, GATE_W), const),
          pl.BlockSpec((1, GDN_V), const),
          pl.BlockSpec((1, SSM_INNER), const),
          pl.BlockSpec((1, SSM_INNER), const),
          pl.BlockSpec((tm, tm), const),
          pl.BlockSpec((3 * GATE_W, EXP_W), const),
          pl.BlockSpec((2 * LANE, LANE), const),
          pl.BlockSpec((2 * LANE, LANE), const),
      ],
      out_specs=pl.BlockSpec((1, tm, D_MIX), row),
      out_shape=jax.ShapeDtypeStruct((b, l, D_MIX), BF16),
      scratch_shapes=[
          pltpu.VMEM((HEADS // 2, LANE, LANE), F32),
          pltpu.VMEM((SSM_BC, SSM_INNER), F32),
          pltpu.VMEM((tm, EXP_W), F32),
      ] + [pltpu.VMEM((tm, 512), F32)] * 9,
      compiler_params=pltpu.CompilerParams(
          dimension_semantics=("arbitrary", "arbitrary"), vmem_limit_bytes=VMEM_LIMIT),
      name="mixer",
  )(qkv, xbc, zz, gates, gbias, alog, gnw_row, snw_row, sd_row, lt_bd, e3, ones64x2, ones128x2)


def _out_kernel(x_ref, mix_ref, mod_ref, ln_ref, fnw_ref, wo_ref, w1_ref, w2_ref, o_ref, *, final_norm):
  x = x_ref[0]
  mod = mod_ref[0]
  x1 = x + mod[2:3] * jnp.dot(mix_ref[0], wo_ref[...], preferred_element_type=F32)
  ms = jnp.mean(x1 * x1, axis=-1, keepdims=True)
  h = (x1 * lax.rsqrt(ms + EPS) * (ln_ref[...] * (1.0 + mod[4:5])) + mod[3:4]).astype(BF16)
  acc = jnp.zeros(x.shape, F32)
  for j in range(D_FF // FF_BLOCK):
    a = jnp.maximum(jnp.dot(h, w1_ref[:, j * FF_BLOCK:(j + 1) * FF_BLOCK],
                            preferred_element_type=F32), 0.0)
    acc = acc + jnp.dot((a * a).astype(BF16), w2_ref[j * FF_BLOCK:(j + 1) * FF_BLOCK, :],
                        preferred_element_type=F32)
  x2 = x1 + mod[5:6] * acc
  if final_norm:
    ms2 = jnp.mean(x2 * x2, axis=-1, keepdims=True)
    x2 = x2 * lax.rsqrt(ms2 + EPS) * fnw_ref[...]
  o_ref[0] = x2


def _out_mlp(x, mix, mod, ln2_w, fnw, wo, w1, w2, *, final_norm):
  b, l, _ = x.shape
  tm = TM_OUT
  const = lambda i, t: (0, 0)
  row = lambda i, t: (i, t, 0)
  return pl.pallas_call(
      functools.partial(_out_kernel, final_norm=final_norm),
      grid=(b, l // tm),
      in_specs=[
          pl.BlockSpec((1, tm, D_MODEL), row),
          pl.BlockSpec((1, tm, D_MIX), row),
          pl.BlockSpec((1, N_MOD, D_MODEL), lambda i, t: (i, 0, 0)),
          pl.BlockSpec((1, D_MODEL), const),
          pl.BlockSpec((1, D_MODEL), const),
          pl.BlockSpec((D_MIX, D_MODEL), const),
          pl.BlockSpec((D_MODEL, D_FF), const),
          pl.BlockSpec((D_FF, D_MODEL), const),
      ],
      out_specs=pl.BlockSpec((1, tm, D_MODEL), row),
      out_shape=jax.ShapeDtypeStruct((b, l, D_MODEL), F32),
      compiler_params=pltpu.CompilerParams(
          dimension_semantics=("arbitrary", "arbitrary"), vmem_limit_bytes=VMEM_LIMIT),
      name="out_mlp",
  )(x, mix, mod, ln2_w, fnw, wo, w1, w2)


def _expand_matrix():
  src = jnp.arange(GATE_W)[:, None]
  dst = jnp.arange(EXP_W)[None, :]
  e = (src == (dst // 512) * HEADS + (dst % 512) // HEAD_W).astype(BF16)
  return jnp.concatenate([e, e, e], axis=0)


def _group_ones(width):
  i = jnp.arange(LANE)
  m = ((i[:, None] // width) == (i[None, :] // width)).astype(BF16)
  return jnp.concatenate([m, m], axis=0)


def _chunk_tril(tm):
  i = jnp.arange(tm)
  return ((i[:, None] >= i[None, :]) & ((i[:, None] // CHUNK) == (i[None, :] // CHUNK))).astype(F32)


def kernel(x, c, ln1_w, ln2_w, ada_w, ada_b, w_in, gdn_conv_w, gdn_A_log, gdn_dt_bias, gdn_norm_w,
           ssm_conv_w, ssm_conv_b, ssm_A_log, ssm_dt_bias, ssm_D, ssm_norm_w, w_out, w_ff1, w_ff2,
           final_norm_w):
  depth = ln1_w.shape[0]
  b = x.shape[0]
  ones64x2 = _group_ones(HEAD_W)
  ones128x2 = _group_ones(LANE)
  e3 = _expand_matrix()
  lt_bd = _chunk_tril(TM_MIX)
  c_qkv, c_z, c_b, c_a, c_xbc, c_sz, c_dt = (
      GDN_CONV_CH, GDN_V, HEADS, HEADS, SSM_CONV_CH, SSM_INNER, HEADS)
  o_z = c_qkv
  o_b = o_z + c_z
  o_a = o_b + c_b
  o_xbc = o_a + c_a
  o_sz = o_xbc + c_xbc
  o_dt = o_sz + c_sz
  for layer in range(depth):
    mod = _ada_mod(c, ada_w[layer], ada_b[layer]).reshape(b, N_MOD, D_MODEL)
    w = w_in[layer]
    w_dt = w[:, o_dt:o_dt + c_dt]
    w_in_p = jnp.concatenate(
        [w[:, 0:c_qkv], w[:, o_xbc:o_xbc + c_xbc], w[:, o_z:o_z + c_z], w[:, o_sz:o_sz + c_sz],
         w[:, o_b:o_b + c_b], w[:, o_a:o_a + c_a], w_dt, w_dt,
         jnp.zeros((D_MODEL, GATE_W - 4 * HEADS), F32)], axis=1).astype(BF16)
    conv_w = jnp.concatenate([gdn_conv_w[layer], ssm_conv_w[layer]], axis=1)
    conv_b = jnp.concatenate([jnp.zeros((GDN_CONV_CH,), F32), ssm_conv_b[layer]])[None, :]
    zeros8 = jnp.zeros((HEADS,), F32)
    pad = jnp.zeros((GATE_W - 4 * HEADS,), F32)
    gbias = jnp.concatenate([zeros8, gdn_dt_bias[layer], ssm_dt_bias[layer], ssm_dt_bias[layer], pad])[None, :]
    alog = jnp.concatenate([zeros8, gdn_A_log[layer], zeros8, ssm_A_log[layer], pad])[None, :]
    gnw_row = jnp.tile(gdn_norm_w[layer], HEADS)[None, :]
    snw_row = ssm_norm_w[layer][None, :]
    sd_row = jnp.repeat(ssm_D[layer], HEAD_W)[None, :]

    qkv, xbc, zz, gates = _in_proj(x, mod, ln1_w[layer][None, :], w_in_p, conv_w, conv_b, ones64x2)
    mix = _mixer(qkv, xbc, zz, gates, gbias, alog, gnw_row, snw_row, sd_row, lt_bd, e3,
                 ones64x2, ones128x2)
    x = _out_mlp(x, mix, mod, ln2_w[layer][None, :], final_norm_w[None, :],
                 w_out[layer].astype(BF16), w_ff1[layer].astype(BF16), w_ff2[layer].astype(BF16),
                 final_norm=layer == depth - 1)
  return x
```

```python
import functools

import jax
import jax.numpy as jnp
from jax import lax
from jax.experimental import pallas as pl
from jax.experimental.pallas import tpu as pltpu

F32 = jnp.float32
BF16 = jnp.bfloat16
HIGHEST = lax.Precision.HIGHEST

EPS = 1e-6
D_MODEL = 1024
N_MOD = 6
CHUNK = 64
CONV_K = 4
HEADS = 8
HEAD_W = 64
GDN_QK = 512
GDN_V = 512
SSM_INNER = 512
SSM_BC = 128
SSM_GROUP_W = 256
D_MIX = GDN_V + SSM_INNER
D_FF = 4 * D_MODEL
GDN_CONV_CH = 2 * GDN_QK + GDN_V
SSM_CONV_CH = SSM_INNER + 2 * SSM_BC
CONV_CH = GDN_CONV_CH + SSM_CONV_CH
GATE_W = 128
GATE_COPY_W = 4 * HEADS
PROJ_W = CONV_CH + GDN_V + SSM_INNER + GATE_W
EXP_W = 4 * 512

LANE = 128
HALO = 8
VMEM_LIMIT = 56 * 1024 * 1024

TM_PROJ = 512
TM_MIX = 256
TM_OUT = 512
FF_BLOCK = 1024


def _mm(a, b):
  return jnp.dot(a.astype(BF16), b.astype(BF16), preferred_element_type=F32)


def _mm_nt(a, b):
  return lax.dot_general(a.astype(BF16), b.astype(BF16), (((1,), (1,)), ((), ())),
                         preferred_element_type=F32)


def _mm_tn(a, b):
  return lax.dot_general(a.astype(BF16), b.astype(BF16), (((0,), (0,)), ((), ())),
                         preferred_element_type=F32)


def _split3(x):
  x1 = x.astype(BF16)
  r = x - x1.astype(F32)
  x2 = r.astype(BF16)
  x3 = (r - x2.astype(F32)).astype(BF16)
  return x1, x2, x3


def _group_sum(x, ones):
  outs = [_mm(x[:, c * LANE:(c + 1) * LANE], ones) for c in range(x.shape[1] // LANE)]
  return outs[0] if len(outs) == 1 else jnp.concatenate(outs, axis=1)


def _silu(x):
  return x * jax.nn.sigmoid(x)


def _softplus(x):
  return jnp.maximum(x, 0.0) + jnp.log1p(jnp.exp(-jnp.abs(x)))


def _mod_kernel(c_ref, w_ref, b_ref, o_ref):
  c = c_ref[...]
  o_ref[...] = jnp.dot(_silu(c), w_ref[...], precision=HIGHEST,
                       preferred_element_type=F32) + b_ref[...]


def _ada_mod(c, ada_w, ada_b):
  b = c.shape[0]
  n = ada_w.shape[1]
  blk = D_MODEL
  return pl.pallas_call(
      _mod_kernel,
      grid=(n // blk,),
      in_specs=[
          pl.BlockSpec((b, D_MODEL), lambda j: (0, 0)),
          pl.BlockSpec((D_MODEL, blk), lambda j: (0, j)),
          pl.BlockSpec((1, blk), lambda j: (0, j)),
      ],
      out_specs=pl.BlockSpec((b, blk), lambda j: (0, j)),
      out_shape=jax.ShapeDtypeStruct((b, n), F32),
      compiler_params=pltpu.CompilerParams(
          dimension_semantics=("arbitrary",), vmem_limit_bytes=VMEM_LIMIT),
      name="ada_mod",
  )(c, ada_w, ada_b.reshape(1, n))


def _proj_kernel(x_ref, mod_ref, ln_ref, w_ref, cw_ref, cb_ref, ones_ref,
                 qkv_ref, xbc_ref, zz_ref, gate_ref, cbuf):
  tm = x_ref.shape[1]

  @pl.when(pl.program_id(1) == 0)
  def _():
    cbuf[0:HALO, :] = jnp.zeros((HALO, CONV_CH), F32)

  x = x_ref[0]
  mod = mod_ref[0]
  ms = jnp.mean(x * x, axis=-1, keepdims=True)
  h = x * lax.rsqrt(ms + EPS) * (ln_ref[...] * (1.0 + mod[1:2])) + mod[0:1]
  proj = jnp.dot(h.astype(BF16), w_ref[...], preferred_element_type=F32)

  cbuf[HALO:HALO + tm, :] = proj[:, :CONV_CH]
  cw = cw_ref[...]
  hist = cbuf[...]
  conv = cb_ref[...] + cw[CONV_K - 1:CONV_K] * hist[HALO:, :]
  for j in range(CONV_K - 1):
    conv = conv + cw[j:j + 1] * pltpu.roll(hist, CONV_K - 1 - j, axis=0)[HALO:, :]
  cbuf[0:HALO, :] = hist[tm:tm + HALO, :]
  act = _silu(conv)

  ones2 = ones_ref[...]
  for s in range(2):
    u = act[:, s * GDN_QK:(s + 1) * GDN_QK]
    qkv_ref[0, :, s * GDN_QK:(s + 1) * GDN_QK] = u * lax.rsqrt(_group_sum(u * u, ones2) + EPS)
  qkv_ref[0, :, 2 * GDN_QK:GDN_CONV_CH] = act[:, 2 * GDN_QK:GDN_CONV_CH]
  xbc_ref[0] = act[:, GDN_CONV_CH:CONV_CH]
  zz_ref[0] = proj[:, CONV_CH:CONV_CH + D_MIX]
  gate_ref[0] = proj[:, CONV_CH + D_MIX:PROJ_W]


def _in_proj(x, mod, ln1_w, w_in_p, conv_w, conv_b, ones64):
  b, l, _ = x.shape
  tm = TM_PROJ
  const = lambda i, t: (0, 0)
  row = lambda i, t: (i, t, 0)
  return pl.pallas_call(
      _proj_kernel,
      grid=(b, l // tm),
      in_specs=[
          pl.BlockSpec((1, tm, D_MODEL), row),
          pl.BlockSpec((1, N_MOD, D_MODEL), lambda i, t: (i, 0, 0)),
          pl.BlockSpec((1, D_MODEL), const),
          pl.BlockSpec((D_MODEL, PROJ_W), const),
          pl.BlockSpec((CONV_K, CONV_CH), const),
          pl.BlockSpec((1, CONV_CH), const),
          pl.BlockSpec((LANE, LANE), const),
      ],
      out_specs=[
          pl.BlockSpec((1, tm, GDN_CONV_CH), row),
          pl.BlockSpec((1, tm, SSM_CONV_CH), row),
          pl.BlockSpec((1, tm, D_MIX), row),
          pl.BlockSpec((1, tm, GATE_W), row),
      ],
      out_shape=[
          jax.ShapeDtypeStruct((b, l, GDN_CONV_CH), F32),
          jax.ShapeDtypeStruct((b, l, SSM_CONV_CH), F32),
          jax.ShapeDtypeStruct((b, l, D_MIX), F32),
          jax.ShapeDtypeStruct((b, l, GATE_W), F32),
      ],
      scratch_shapes=[pltpu.VMEM((tm + HALO, CONV_CH), F32)],
      compiler_params=pltpu.CompilerParams(
          dimension_semantics=("arbitrary", "arbitrary"), vmem_limit_bytes=VMEM_LIMIT),
      name="in_proj",
  )(x, mod, ln1_w, w_in_p, conv_w, conv_b, ones64)


def _block_diag(y, lo):
  return jnp.concatenate([jnp.where(lo, y, 0.0), jnp.where(lo, 0.0, y)], axis=0)


def _mixer_kernel(qkv_ref, xbc_ref, zz_ref, gate_ref, gbias_ref, alog_ref, gnw_ref, snw_ref,
                  sd_ref, lt_ref, e3_ref, ones64_ref, ones128_ref, out_ref, sb_ref, hs_ref):
  tm = qkv_ref.shape[1]
  n_chunks = tm // CHUNK
  n_pairs = HEADS // 2

  @pl.when(pl.program_id(1) == 0)
  def _():
    sb_ref[...] = jnp.zeros(sb_ref.shape, F32)
    hs_ref[...] = jnp.zeros(hs_ref.shape, F32)

  row = lax.broadcasted_iota(jnp.int32, (CHUNK, LANE), 0)
  lane = lax.broadcasted_iota(jnp.int32, (CHUNK, LANE), 1)
  pos = lane & (HEAD_W - 1)
  lo = lane < HEAD_W
  eye2 = jnp.where(pos == row, 1.0, 0.0)
  strict2 = pos < row
  row8 = lax.broadcasted_iota(jnp.int32, (CHUNK, 512), 0)
  pos8 = lax.broadcasted_iota(jnp.int32, (CHUNK, 512), 1) & (HEAD_W - 1)
  incl8 = pos8 <= row8
  eye8 = pos8 == row8
  rr = lax.broadcasted_iota(jnp.int32, (LANE, LANE), 0)
  ll = lax.broadcasted_iota(jnp.int32, (LANE, LANE), 1)
  bd_mask = (rr >> 6) == (ll >> 6)
  rg = lax.broadcasted_iota(jnp.int32, (SSM_BC, SSM_INNER), 0)
  lg = lax.broadcasted_iota(jnp.int32, (SSM_BC, SSM_INNER), 1)
  grp_mask = (rg >> 6) == (lg >> 8)
  ones3 = jnp.ones((CHUNK, 3 * CHUNK), BF16)
  scale = HEAD_W ** -0.5

  raw = gate_ref[0]
  lane_g = lax.broadcasted_iota(jnp.int32, (tm, GATE_W), 1)
  slot = (lane_g >> 3) & 3
  copy = lane_g >> 5
  slot1 = (lax.broadcasted_iota(jnp.int32, (1, GATE_W), 1) >> 3) & 3
  neg_a = -jnp.exp(alog_ref[...])
  gmult = jnp.where(slot1 == 1, neg_a, jnp.where(slot1 == 3, neg_a, jnp.where(slot1 == 2, 1.0, 0.0)))
  pre = jnp.where(slot == 0, jax.nn.sigmoid(raw), _softplus(raw + gbias_ref[...]) * gmult)
  cs3 = jnp.dot(lt_ref[...], jnp.concatenate(_split3(pre), axis=1), preferred_element_type=F32)
  cs = cs3[:, 0:GATE_W] + cs3[:, GATE_W:2 * GATE_W] + cs3[:, 2 * GATE_W:3 * GATE_W]
  comb = jnp.where(slot == 1, cs, jnp.where(slot == 3, cs, pre))
  r1 = comb - comb.astype(BF16).astype(F32)
  r2 = r1 - r1.astype(BF16).astype(F32)
  pieces = jnp.where(copy == 0, comb, jnp.where(copy == 1, r1, r2)).astype(BF16)
  ex = jnp.dot(pieces, e3_ref[...], preferred_element_type=F32)

  def row_bcast(col_bc):
    d = jnp.where(eye8, col_bc, 0.0)
    return jnp.dot(ones3, jnp.concatenate(_split3(d), axis=0), preferred_element_type=F32)

  chains = []
  gl_rows = []
  for c in range(n_chunks):
    rows = slice(c * CHUNK, (c + 1) * CHUNK)
    beta_bc = ex[rows, 0:512]
    g_bc = ex[rows, 512:1024]
    dec = jnp.exp(jnp.where(incl8, g_bc - row_bcast(g_bc), -jnp.inf))
    exp_g = jnp.exp(g_bc)
    g_last = g_bc[CHUNK - 1:CHUNK, :]
    exp_gl_g = jnp.exp(g_last - g_bc)
    gl_rows.append(jnp.exp(g_last))
    for p in range(n_pairs):
      cols = slice(p * LANE, (p + 1) * LANE)
      q2 = qkv_ref[0, rows, p * LANE:(p + 1) * LANE] * scale
      k2 = qkv_ref[0, rows, GDN_QK + p * LANE:GDN_QK + (p + 1) * LANE]
      v2 = qkv_ref[0, rows, 2 * GDN_QK + p * LANE:2 * GDN_QK + (p + 1) * LANE]
      kb2 = k2 * beta_bc[:, cols]
      chains.append(dict(rows=rows, cols=cols, q2=q2, k2=k2, kb2=kb2, d2=dec[:, cols],
                         vb2=v2 * beta_bc[:, cols], kbg2=kb2 * exp_g[:, cols],
                         qg2=q2 * exp_g[:, cols], kd2=k2 * exp_gl_g[:, cols],
                         rows2=slice(c * LANE, (c + 1) * LANE)))
  aq = [_mm_nt(jnp.concatenate([ch["kb2"], ch["q2"]], axis=0), _block_diag(ch["k2"], lo))
        for ch in chains]
  qkd = [a[CHUNK:2 * CHUNK] * ch["d2"] for ch, a in zip(chains, aq)]
  n1 = [jnp.where(strict2, -(a[0:CHUNK] * ch["d2"]), 0.0) for ch, a in zip(chains, aq)]
  xk = [_mm(x, _block_diag(x, lo)) for x in n1]
  pk = [eye2 + x for x in n1]
  for _ in range(4):
    both = [_mm(x, jnp.concatenate([_block_diag(x, lo), _block_diag(p_, lo)], axis=1))
            for x, p_ in zip(xk, pk)]
    xk = [b_[:, 0:LANE] for b_ in both]
    pk = [p_ + b_[:, LANE:2 * LANE] for p_, b_ in zip(pk, both)]
  t2 = [p_ + _mm(x, _block_diag(p_, lo)) for x, p_ in zip(xk, pk)]
  res = [eye2 - t + _mm(x, _block_diag(t, lo)) for x, t in zip(n1, t2)]
  t2 = [t + _mm(t, _block_diag(e, lo)) for t, e in zip(t2, res)]
  wu = [_mm(t, jnp.concatenate([_block_diag(ch["kbg2"], lo), _block_diag(ch["vb2"], lo)], axis=1))
        for ch, t in zip(chains, t2)]
  kwu = [_mm_tn(ch["kd2"], x) for ch, x in zip(chains, wu)]
  qwu = [_mm(a, jnp.concatenate([_block_diag(x[:, 0:LANE], lo), _block_diag(x[:, LANE:2 * LANE], lo)],
                                axis=1)) for a, x in zip(qkd, wu)]
  sm = [jnp.where(bd_mask, -kx[:, 0:LANE], 0.0) for kx in kwu]
  sc = [jnp.where(bd_mask, kx[:, LANE:2 * LANE], 0.0) for kx in kwu]
  qe = [ch["qg2"] - qx[:, 0:LANE] for ch, qx in zip(chains, qwu)]
  oc = [qx[:, LANE:2 * LANE] for qx in qwu]

  ssd = []
  for c in range(n_chunks):
    rows = slice(c * CHUNK, (c + 1) * CHUNK)
    dt_bc = ex[rows, 1024:1536]
    ac_bc = ex[rows, 1536:2048]
    xs = xbc_ref[0, rows, 0:SSM_INNER]
    bm = xbc_ref[0, rows, SSM_INNER:SSM_INNER + SSM_BC]
    cm = xbc_ref[0, rows, SSM_INNER + SSM_BC:SSM_CONV_CH]
    ldec = jnp.exp(jnp.where(incl8, ac_bc - row_bcast(ac_bc), -jnp.inf))
    bm_lo = jnp.where(lo, bm, 0.0)
    bm_hi = jnp.where(lo, 0.0, bm)
    r8 = jnp.concatenate([bm_lo] * (HEADS // 2) + [bm_hi] * (HEADS // 2), axis=0)
    scores = _mm_nt(cm, r8) * ldec
    xdt = xs * dt_bc
    yd = [_mm(scores[:, p * LANE:(p + 1) * LANE], _block_diag(xdt[:, p * LANE:(p + 1) * LANE], lo))
          for p in range(n_pairs)]
    a_last = ac_bc[CHUNK - 1:CHUNK, :]
    ssd.append(dict(yd=jnp.concatenate(yd, axis=1) + sd_ref[...] * xs,
                    xw=xdt * jnp.exp(a_last - ac_bc), ea=jnp.exp(ac_bc), al=jnp.exp(a_last),
                    bm=bm, cm=cm))

  sbs = [sb_ref[p] for p in range(n_pairs)]
  hs = hs_ref[...]
  for c in range(n_chunks):
    rows = slice(c * CHUNK, (c + 1) * CHUNK)
    o = []
    for p in range(n_pairs):
      i = c * n_pairs + p
      r = _mm(jnp.concatenate([sm[i], qe[i]], axis=0), sbs[p])
      o.append(oc[i] + r[LANE:LANE + CHUNK])
      sbs[p] = sbs[p] * gl_rows[c][:, p * LANE:(p + 1) * LANE] + r[0:LANE] + sc[i]
    o = jnp.concatenate(o, axis=1)
    s = ssd[c]
    y = s["yd"] + _mm(s["cm"], hs) * s["ea"]
    hs = hs * s["al"] + jnp.where(grp_mask, _mm_tn(s["bm"], s["xw"]), 0.0)

    ms = _group_sum(o * o, ones64_ref[...]) * (1.0 / HEAD_W)
    out_ref[0, rows, 0:GDN_V] = (o * lax.rsqrt(ms + EPS) * gnw_ref[...]
                                 * _silu(zz_ref[0, rows, 0:GDN_V])).astype(out_ref.dtype)
    yz = y * _silu(zz_ref[0, rows, GDN_V:D_MIX])
    sq = yz * yz
    ss = [_group_sum(sq[:, g * SSM_GROUP_W:g * SSM_GROUP_W + LANE]
                     + sq[:, g * SSM_GROUP_W + LANE:(g + 1) * SSM_GROUP_W], ones128_ref[...])
          for g in range(2)]
    ms = jnp.concatenate([ss[0], ss[0], ss[1], ss[1]], axis=1) * (1.0 / SSM_GROUP_W)
    out_ref[0, rows, GDN_V:D_MIX] = (yz * lax.rsqrt(ms + EPS) * snw_ref[...]).astype(out_ref.dtype)
  for p in range(n_pairs):
    sb_ref[p] = sbs[p]
  hs_ref[...] = hs


def _mixer(qkv, xbc, zz, gates, gbias, alog, gnw_row, snw_row, sd_row, lt_bd, e3, ones64, ones128):
  b, l, _ = qkv.shape
  tm = TM_MIX
  const = lambda i, t: (0, 0)
  row = lambda i, t: (i, t, 0)
  return pl.pallas_call(
      _mixer_kernel,
      grid=(b, l // tm),
      in_specs=[
          pl.BlockSpec((1, tm, GDN_CONV_CH), row),
          pl.BlockSpec((1, tm, SSM_CONV_CH), row),
          pl.BlockSpec((1, tm, D_MIX), row),
          pl.BlockSpec((1, tm, GATE_W), row),
          pl.BlockSpec((1, GATE_W), const),
          pl.BlockSpec((1, GATE_W), const),
          pl.BlockSpec((1, GDN_V), const),
          pl.BlockSpec((1, SSM_INNER), const),
          pl.BlockSpec((1, SSM_INNER), const),
          pl.BlockSpec((tm, tm), const),
          pl.BlockSpec((GATE_W, EXP_W), const),
          pl.BlockSpec((LANE, LANE), const),
          pl.BlockSpec((LANE, LANE), const),
      ],
      out_specs=pl.BlockSpec((1, tm, D_MIX), row),
      out_shape=jax.ShapeDtypeStruct((b, l, D_MIX), BF16),
      scratch_shapes=[
          pltpu.VMEM((HEADS // 2, LANE, LANE), F32),
          pltpu.VMEM((SSM_BC, SSM_INNER), F32),
      ],
      compiler_params=pltpu.CompilerParams(
          dimension_semantics=("arbitrary", "arbitrary"), vmem_limit_bytes=VMEM_LIMIT),
      name="mixer",
  )(qkv, xbc, zz, gates, gbias, alog, gnw_row, snw_row, sd_row, lt_bd, e3, ones64, ones128)


def _out_kernel(x_ref, mix_ref, mod_ref, ln_ref, fnw_ref, wo_ref, w1_ref, w2_ref, o_ref, *, final_norm):
  x = x_ref[0]
  mod = mod_ref[0]
  x1 = x + mod[2:3] * jnp.dot(mix_ref[0], wo_ref[...], preferred_element_type=F32)
  ms = jnp.mean(x1 * x1, axis=-1, keepdims=True)
  h = (x1 * lax.rsqrt(ms + EPS) * (ln_ref[...] * (1.0 + mod[4:5])) + mod[3:4]).astype(BF16)
  acc = jnp.zeros(x.shape, F32)
  for j in range(D_FF // FF_BLOCK):
    a = jnp.maximum(jnp.dot(h, w1_ref[:, j * FF_BLOCK:(j + 1) * FF_BLOCK],
                            preferred_element_type=F32), 0.0)
    acc = acc + jnp.dot((a * a).astype(BF16), w2_ref[j * FF_BLOCK:(j + 1) * FF_BLOCK, :],
                        preferred_element_type=F32)
  x2 = x1 + mod[5:6] * acc
  if final_norm:
    ms2 = jnp.mean(x2 * x2, axis=-1, keepdims=True)
    x2 = x2 * lax.rsqrt(ms2 + EPS) * fnw_ref[...]
  o_ref[0] = x2


def _out_mlp(x, mix, mod, ln2_w, fnw, wo, w1, w2, *, final_norm):
  b, l, _ = x.shape
  tm = TM_OUT
  const = lambda i, t: (0, 0)
  row = lambda i, t: (i, t, 0)
  return pl.pallas_call(
      functools.partial(_out_kernel, final_norm=final_norm),
      grid=(b, l // tm),
      in_specs=[
          pl.BlockSpec((1, tm, D_MODEL), row),
          pl.BlockSpec((1, tm, D_MIX), row),
          pl.BlockSpec((1, N_MOD, D_MODEL), lambda i, t: (i, 0, 0)),
          pl.BlockSpec((1, D_MODEL), const),
          pl.BlockSpec((1, D_MODEL), const),
          pl.BlockSpec((D_MIX, D_MODEL), const),
          pl.BlockSpec((D_MODEL, D_FF), const),
          pl.BlockSpec((D_FF, D_MODEL), const),
      ],
      out_specs=pl.BlockSpec((1, tm, D_MODEL), row),
      out_shape=jax.ShapeDtypeStruct((b, l, D_MODEL), F32),
      compiler_params=pltpu.CompilerParams(
          dimension_semantics=("arbitrary", "arbitrary"), vmem_limit_bytes=VMEM_LIMIT),
      name="out_mlp",
  )(x, mix, mod, ln2_w, fnw, wo, w1, w2)


def _expand_matrix():
  src = jnp.arange(GATE_W)[:, None]
  dst = jnp.arange(EXP_W)[None, :]
  hit = (src % GATE_COPY_W == (dst // 512) * HEADS + (dst % 512) // HEAD_W) & (src < 3 * GATE_COPY_W)
  return hit.astype(BF16)


def _group_ones(width):
  i = jnp.arange(LANE)
  return ((i[:, None] // width) == (i[None, :] // width)).astype(BF16)


def _chunk_tril(tm):
  i = jnp.arange(tm)
  return ((i[:, None] >= i[None, :]) & ((i[:, None] // CHUNK) == (i[None, :] // CHUNK))).astype(BF16)


def kernel(x, c, ln1_w, ln2_w, ada_w, ada_b, w_in, gdn_conv_w, gdn_A_log, gdn_dt_bias, gdn_norm_w,
           ssm_conv_w, ssm_conv_b, ssm_A_log, ssm_dt_bias, ssm_D, ssm_norm_w, w_out, w_ff1, w_ff2,
           final_norm_w):
  depth = ln1_w.shape[0]
  b = x.shape[0]
  ones64 = _group_ones(HEAD_W)
  ones128 = _group_ones(LANE)
  e3 = _expand_matrix()
  lt_bd = _chunk_tril(TM_MIX)
  c_qkv, c_z, c_b, c_a, c_xbc, c_sz, c_dt = (
      GDN_CONV_CH, GDN_V, HEADS, HEADS, SSM_CONV_CH, SSM_INNER, HEADS)
  o_z = c_qkv
  o_b = o_z + c_z
  o_a = o_b + c_b
  o_xbc = o_a + c_a
  o_sz = o_xbc + c_xbc
  o_dt = o_sz + c_sz
  for layer in range(depth):
    mod = _ada_mod(c, ada_w[layer], ada_b[layer]).reshape(b, N_MOD, D_MODEL)
    w = w_in[layer]
    w_dt = w[:, o_dt:o_dt + c_dt]
    w_gate = jnp.concatenate([w[:, o_b:o_b + c_b], w[:, o_a:o_a + c_a], w_dt, w_dt], axis=1)
    w_in_p = jnp.concatenate(
        [w[:, 0:c_qkv], w[:, o_xbc:o_xbc + c_xbc], w[:, o_z:o_z + c_z], w[:, o_sz:o_sz + c_sz],
         w_gate, w_gate, w_gate, jnp.zeros((D_MODEL, GATE_W - 3 * GATE_COPY_W), F32)],
        axis=1).astype(BF16)
    conv_w = jnp.concatenate([gdn_conv_w[layer], ssm_conv_w[layer]], axis=1)
    conv_b = jnp.concatenate([jnp.zeros((GDN_CONV_CH,), F32), ssm_conv_b[layer]])[None, :]
    zeros8 = jnp.zeros((HEADS,), F32)
    pad = jnp.zeros((GATE_W - 3 * GATE_COPY_W,), F32)
    gbias = jnp.concatenate(
        [zeros8, gdn_dt_bias[layer], ssm_dt_bias[layer], ssm_dt_bias[layer]] * 3 + [pad])[None, :]
    alog = jnp.concatenate([zeros8, gdn_A_log[layer], zeros8, ssm_A_log[layer]] * 3 + [pad])[None, :]
    gnw_row = jnp.tile(gdn_norm_w[layer], HEADS)[None, :]
    snw_row = ssm_norm_w[layer][None, :]
    sd_row = jnp.repeat(ssm_D[layer], HEAD_W)[None, :]

    qkv, xbc, zz, gates = _in_proj(x, mod, ln1_w[layer][None, :], w_in_p, conv_w, conv_b, ones64)
    mix = _mixer(qkv, xbc, zz, gates, gbias, alog, gnw_row, snw_row, sd_row, lt_bd, e3,
                 ones64, ones128)
    x = _out_mlp(x, mix, mod, ln2_w[layer][None, :], final_norm_w[None, :],
                 w_out[layer].astype(BF16), w_ff1[layer].astype(BF16), w_ff2[layer].astype(BF16),
                 final_norm=layer == depth - 1)
  return x
```
